```python
import jax, jax.numpy as jnp
from jax import lax
import numpy as np

D_MODEL = 2048
BATCH = 2
SEQ = 16384
DEPTH = 1
DEC_BATCH = 16
DEC_SEQ = 16
PAST_LEN = 1024

CHUNK = 64
D_MIX = D_MODEL
GLA_HEADS = 4
GLA_WIDTH = D_MIX // 2
GLA_DV = GLA_WIDTH // GLA_HEADS
GLA_DK = GLA_DV // 2
GLA_KW = GLA_HEADS * GLA_DK
GATE_RANK = 16
GATE_NORM = 16.0
POOL_WIDTH = D_MIX - GLA_WIDTH
POOL_WINDOWS = (2, 4, 8, 16)
POOL_GROUPS = len(POOL_WINDOWS)
POOL_GC = POOL_WIDTH // POOL_GROUPS
POOL_HIST = max(POOL_WINDOWS) - 1
SPLITS = (GLA_KW, 2 * GLA_KW, 2 * GLA_KW + GLA_WIDTH, 2 * GLA_KW + 2 * GLA_WIDTH,
          2 * GLA_KW + 2 * GLA_WIDTH + GATE_RANK)
D_IN = 2 * GLA_KW + 2 * GLA_WIDTH + GATE_RANK + POOL_WIDTH
N_EXPERTS = 32
TOP_K = 4
D_FF = D_MODEL
SWIGLU_LIMIT = 7.0
SWIGLU_ALPHA = 1.702
MOE_BLOCK = 128
DN_ALPHA = (2 * DEPTH) ** 0.25
DN_BETA = (8 * DEPTH) ** -0.25
EPS = 1e-5

kernel_name = 'hybrid_gla_pool_moe_stream_step'


def layer_norm(x):
    xf = x.astype(jnp.float32)
    mu = jnp.mean(xf, -1, keepdims=True)
    var = jnp.mean(jnp.square(xf - mu), -1, keepdims=True)
    return ((xf - mu) * lax.rsqrt(var + EPS)).astype(x.dtype)


def post_norm(z, g, b):
    return layer_norm(z) * g + b


def ada_modulation(c, w_ada, b_ada):
    mod = jax.nn.silu(c) @ w_ada + b_ada
    return jnp.split(mod[:, None, :], 6, axis=-1)


def modulate(x, shift, scale):
    return layer_norm(x) * (1.0 + scale) + shift


def mixer_inputs(h, w_in, w_gate_lr, b_gate_lr):
    B, L, _ = h.shape
    q, k, v, r, glr, u = jnp.split(h @ w_in, SPLITS, axis=-1)
    q = q.reshape(B, L, GLA_HEADS, GLA_DK) * GLA_DK ** -0.5
    k = k.reshape(B, L, GLA_HEADS, GLA_DK)
    v = v.reshape(B, L, GLA_HEADS, GLA_DV)
    lg = jax.nn.log_sigmoid((glr @ w_gate_lr + b_gate_lr).astype(jnp.float32)) / GATE_NORM
    lg = lg.reshape(B, L, GLA_HEADS, GLA_DK)
    return q, k, v, lg, r, u


def gla_block(S, q, k, v, lg):
    f32 = jnp.float32
    L = q.shape[2]
    G = jnp.cumsum(lg.astype(f32), axis=2)
    causal = jnp.tril(jnp.ones((L, L), bool))[:, :, None]
    diff = G[:, :, :, None, :] - G[:, :, None, :, :]
    decay = jnp.exp(jnp.where(causal, diff, -jnp.inf))
    qf, kf, vf = q.astype(f32), k.astype(f32), v.astype(f32)
    scores = jnp.einsum('bhtd,bhsd,bhtsd->bhts', qf, kf, decay)
    o = (jnp.einsum('bhts,bhsv->bhtv', scores, vf)
         + jnp.einsum('bhtd,bhdv->bhtv', qf * jnp.exp(G), S))
    G_last = G[:, :, -1:, :]
    S_new = (jnp.exp(G_last[:, :, 0, :])[..., None] * S
             + jnp.einsum('bhsd,bhsv->bhdv', kf * jnp.exp(G_last - G), vf))
    return S_new, o


def gla_prompt(q, k, v, lg):
    B, S = q.shape[:2]
    nC = S // CHUNK

    def to_blocks(t):
        return t.reshape(B, nC, CHUNK, GLA_HEADS, -1).transpose(1, 0, 3, 2, 4)

    S0 = jnp.zeros((B, GLA_HEADS, GLA_DK, GLA_DV), jnp.float32)
    S_fin, o = lax.scan(lambda s, xs: gla_block(s, *xs), S0,
                        (to_blocks(q), to_blocks(k), to_blocks(v), to_blocks(lg)))
    o = o.transpose(1, 0, 3, 2, 4).reshape(B, S, GLA_HEADS, GLA_DV)
    return o, S_fin


def gla_sample(state, q, k, v, lg):
    tr = lambda t: t.transpose(0, 2, 1, 3)
    S_new, o = gla_block(state.astype(jnp.float32), tr(q), tr(k), tr(v), tr(lg))
    return tr(o), S_new


def pool_mix(u_ext, pos, n_out, w_pool, pool_scale):
    f32 = jnp.float32
    cs = jnp.cumsum(u_ext.astype(f32), axis=-2)
    cs0 = jnp.concatenate([jnp.zeros_like(cs[..., :1, :]), cs], axis=-2)
    u_t = u_ext[..., POOL_HIST:, :].astype(f32)
    outs = []
    for gi, w in enumerate(POOL_WINDOWS):
        sl = slice(gi * POOL_GC, (gi + 1) * POOL_GC)
        lo = POOL_HIST + 1 - w
        win = cs0[..., POOL_HIST + 1:, sl] - cs0[..., lo:lo + n_out, sl]
        cnt = jnp.minimum(pos + 1, w).astype(f32)[..., None]
        outs.append(win / cnt - u_t[..., sl])
    d = jnp.stack(outs, axis=-2)
    y = jnp.einsum('...gc,gcd->...gd', d, w_pool.astype(f32))
    y = y.reshape(*d.shape[:-2], POOL_WIDTH) * pool_scale
    return y.astype(u_ext.dtype)


def pool_prompt(u, w_pool, pool_scale):
    B, S, _ = u.shape
    nC = S // CHUNK
    u_hist = jnp.concatenate([jnp.zeros((B, POOL_HIST, POOL_WIDTH), u.dtype), u], axis=1)
    idx = (jnp.arange(nC) * CHUNK)[:, None] + jnp.arange(POOL_HIST + CHUNK)[None, :]
    y = pool_mix(u_hist[:, idx], jnp.arange(S).reshape(nC, CHUNK), CHUNK, w_pool, pool_scale)
    return y.reshape(B, S, POOL_WIDTH), u[:, S - POOL_HIST:]


def pool_sample(cache, u, w_pool, pool_scale):
    L = u.shape[1]
    u_ext = jnp.concatenate([cache.astype(u.dtype), u], axis=1)
    y = pool_mix(u_ext, PAST_LEN + jnp.arange(L), L, w_pool, pool_scale)
    return y, u_ext[:, u_ext.shape[1] - POOL_HIST:]


def mixer_output(o_gla, r, o_pool, gla_norm_g, w_out):
    B, L = r.shape[:2]
    on = o_gla * lax.rsqrt(jnp.mean(jnp.square(o_gla), -1, keepdims=True) + EPS)
    og = on.reshape(B, L, GLA_WIDTH).astype(r.dtype) * gla_norm_g * jax.nn.silu(r)
    return jnp.concatenate([og, o_pool], axis=-1) @ w_out


def moe(h, router_w, router_b, w_gate, b_gate, w_up, b_up, w_down, b_down):
    T, D = h.shape
    logits = (h @ router_w).astype(jnp.float32) + router_b
    top_v, top_i = lax.top_k(logits, TOP_K)
    gates = jax.nn.softmax(top_v, axis=-1)
    n_assign = T * TOP_K
    flat_e = top_i.reshape(-1)
    order = jnp.argsort(flat_e)
    sorted_e = flat_e[order]
    counts = jnp.bincount(flat_e, length=N_EXPERTS)
    padded = (counts + MOE_BLOCK - 1) // MOE_BLOCK * MOE_BLOCK
    start = jnp.cumsum(counts) - counts
    pad_end = jnp.cumsum(padded)
    pad_start = pad_end - padded
    slot = pad_start[sorted_e] + jnp.arange(n_assign) - start[sorted_e]
    n_blocks = -(-n_assign // MOE_BLOCK) + N_EXPERTS
    tok = order // TOP_K
    slot_tok = jnp.full((n_blocks * MOE_BLOCK,), T, jnp.int32).at[slot].set(tok)
    block_e = jnp.minimum(jnp.searchsorted(pad_end, jnp.arange(n_blocks) * MOE_BLOCK,
                                           side='right'), N_EXPERTS - 1)
    h_pad = jnp.concatenate([h, jnp.zeros((1, D), h.dtype)], axis=0)
    xb = h_pad[slot_tok].reshape(n_blocks, MOE_BLOCK, D)

    def expert_block(args):
        xe, e = args
        g = jnp.minimum(xe @ w_gate[e] + b_gate[e], SWIGLU_LIMIT)
        u = jnp.clip(xe @ w_up[e] + b_up[e], -SWIGLU_LIMIT, SWIGLU_LIMIT)
        a = g * jax.nn.sigmoid(SWIGLU_ALPHA * g) * (u + 1.0)
        return a @ w_down[e] + b_down[e]

    yb = lax.map(expert_block, (xb, block_e)).reshape(-1, D)
    y_assign = yb[slot] * gates.reshape(-1)[order][:, None].astype(yb.dtype)
    return jax.ops.segment_sum(y_assign, tok, num_segments=T)


def encoder_layer(xp, xs, cp, cs, gla_s, pool_c, w_in, w_gate_lr, b_gate_lr, gla_norm_g,
                  w_pool, pool_scale, w_out, w_ada, b_ada, ln1_g, ln1_b, ln2_g, ln2_b,
                  router_w, router_b, w_gate, b_gate, w_up, b_up, w_down, b_down):
    sh_ap, sc_ap, gt_ap, sh_fp, sc_fp, gt_fp = ada_modulation(cp, w_ada, b_ada)
    sh_as, sc_as, gt_as, sh_fs, sc_fs, gt_fs = ada_modulation(cs, w_ada, b_ada)
    q, k, v, lg, r, u = mixer_inputs(modulate(xp, sh_ap, sc_ap), w_in, w_gate_lr, b_gate_lr)
    o_gla, gla_p = gla_prompt(q, k, v, lg)
    o_pool, pool_p = pool_prompt(u, w_pool, pool_scale)
    mix = mixer_output(o_gla, r, o_pool, gla_norm_g, w_out)
    xp = post_norm(DN_ALPHA * xp + gt_ap * mix, ln1_g, ln1_b)
    q, k, v, lg, r, u = mixer_inputs(modulate(xs, sh_as, sc_as), w_in, w_gate_lr, b_gate_lr)
    o_gla, gla_s_new = gla_sample(gla_s, q, k, v, lg)
    o_pool, pool_s_new = pool_sample(pool_c, u, w_pool, pool_scale)
    mix = mixer_output(o_gla, r, o_pool, gla_norm_g, w_out)
    xs = post_norm(DN_ALPHA * xs + gt_as * mix, ln1_g, ln1_b)
    hp = modulate(xp, sh_fp, sc_fp).reshape(-1, D_MODEL)
    hs = modulate(xs, sh_fs, sc_fs).reshape(-1, D_MODEL)
    f = moe(jnp.concatenate([hp, hs], axis=0), router_w, router_b,
            w_gate, b_gate, w_up, b_up, w_down, b_down)
    n_p = hp.shape[0]
    xp = post_norm(DN_ALPHA * xp + gt_fp * f[:n_p].reshape(xp.shape), ln2_g, ln2_b)
    xs = post_norm(DN_ALPHA * xs + gt_fs * f[n_p:].reshape(xs.shape), ln2_g, ln2_b)
    return (xp, xs, gla_p.astype(xp.dtype), gla_s_new.astype(gla_s.dtype),
            pool_p, pool_s_new.astype(pool_c.dtype))


def setup_inputs(seed: int = 0) -> dict:
    key = jax.random.key(seed)
    ks = jax.random.split(key, 32)
    n = lambda i, shape, s=1.0: jax.random.normal(ks[i], shape, jnp.float32) * s
    L_ = DEPTH
    return {
        'x_prompt': n(0, (BATCH, SEQ, D_MODEL)),
        'x_sample': n(1, (DEC_BATCH, DEC_SEQ, D_MODEL)),
        'c_prompt': n(2, (BATCH, D_MODEL)),
        'c_sample': n(3, (DEC_BATCH, D_MODEL)),
        'state_gla': n(4, (L_, DEC_BATCH, GLA_HEADS, GLA_DK, GLA_DV), 2.0),
        'cache_pool': n(5, (L_, DEC_BATCH, POOL_HIST, POOL_WIDTH)),
        'w_in': n(6, (L_, D_MODEL, D_IN), D_MODEL ** -0.5),
        'w_gate_lr': n(7, (L_, GATE_RANK, GLA_KW), GATE_RANK ** -0.5),
        'b_gate_lr': n(8, (L_, GLA_KW), 0.1),
        'gla_norm_g': 1.0 + n(9, (L_, GLA_WIDTH), 0.1),
        'w_pool': n(10, (L_, POOL_GROUPS, POOL_GC, POOL_GC), POOL_GC ** -0.5),
        'pool_scale': 1.0 + n(11, (L_, POOL_WIDTH), 0.1),
        'w_out': n(12, (L_, D_MIX, D_MODEL), DN_BETA * D_MIX ** -0.5),
        'w_ada': n(13, (L_, D_MODEL, 6 * D_MODEL), D_MODEL ** -0.5),
        'b_ada': n(14, (L_, 6 * D_MODEL), 0.02),
        'ln1_g': 1.0 + n(15, (L_, D_MODEL), 0.1),
        'ln1_b': n(16, (L_, D_MODEL), 0.02),
        'ln2_g': 1.0 + n(17, (L_, D_MODEL), 0.1),
        'ln2_b': n(18, (L_, D_MODEL), 0.02),
        'router_w': n(19, (L_, D_MODEL, N_EXPERTS), D_MODEL ** -0.5),
        'router_b': n(20, (L_, N_EXPERTS), 0.01),
        'w_gate': n(21, (L_, N_EXPERTS, D_MODEL, D_FF), D_MODEL ** -0.5),
        'b_gate': n(22, (L_, N_EXPERTS, D_FF), 0.02),
        'w_up': n(23, (L_, N_EXPERTS, D_MODEL, D_FF), D_MODEL ** -0.5),
        'b_up': n(24, (L_, N_EXPERTS, D_FF), 0.02),
        'w_down': n(25, (L_, N_EXPERTS, D_FF, D_MODEL), DN_BETA * D_FF ** -0.5),
        'b_down': n(26, (L_, N_EXPERTS, D_MODEL), 0.02),
    }


def reference(x_prompt, x_sample, c_prompt, c_sample, state_gla, cache_pool, w_in, w_gate_lr,
              b_gate_lr, gla_norm_g, w_pool, pool_scale, w_out, w_ada, b_ada, ln1_g, ln1_b,
              ln2_g, ln2_b, router_w, router_b, w_gate, b_gate, w_up, b_up, w_down, b_down):
    xp, xs = x_prompt, x_sample
    gla_p_all, gla_s_all, pool_p_all, pool_s_all = [], [], [], []
    for l in range(DEPTH):
        xp, xs, gla_p, gla_s, pool_p, pool_s = encoder_layer(
            xp, xs, c_prompt, c_sample, state_gla[l], cache_pool[l], w_in[l], w_gate_lr[l],
            b_gate_lr[l], gla_norm_g[l], w_pool[l], pool_scale[l], w_out[l], w_ada[l], b_ada[l],
            ln1_g[l], ln1_b[l], ln2_g[l], ln2_b[l], router_w[l], router_b[l], w_gate[l],
            b_gate[l], w_up[l], b_up[l], w_down[l], b_down[l])
        gla_p_all.append(gla_p)
        gla_s_all.append(gla_s)
        pool_p_all.append(pool_p)
        pool_s_all.append(pool_s)
    gla_state_prompt = jnp.stack(gla_p_all, 0)
    gla_state_sample = jnp.stack(gla_s_all, 0)
    pool_state_prompt = jnp.stack(pool_p_all, 0)
    pool_state_sample = jnp.stack(pool_s_all, 0)
    return (xp, xs, gla_state_prompt, gla_state_sample, pool_state_prompt, pool_state_sample)
```

```python
import functools

import jax
import jax.numpy as jnp
from jax import lax
from jax.experimental import pallas as pl
from jax.experimental.pallas import tpu as pltpu

F32 = jnp.float32
BF16 = jnp.bfloat16

EPS = 1e-5
GLA_CHUNK = 64
GLA_SUB = 16
GATE_NORM = 16.0
POOL_WINDOWS = (2, 4, 8, 16)
POOL_HIST = max(POOL_WINDOWS) - 1
PAST_LEN = 1024
TOP_K = 4
SWIGLU_LIMIT = 7.0
SWIGLU_ALPHA = 1.702
LANES = 128
MOE_BLK = 512
MOE_TF = 512
VMEM_LIMIT = 56 * 1024 * 1024
NEG = -1e30


def _cparams(sem):
    return pltpu.CompilerParams(dimension_semantics=sem, vmem_limit_bytes=VMEM_LIMIT)


def _dot(a, b):
    return jnp.dot(a, b, preferred_element_type=F32)


def _ln(x):
    mu = jnp.mean(x, -1, keepdims=True)
    xc = x - mu
    var = jnp.mean(xc * xc, -1, keepdims=True)
    return xc * lax.rsqrt(var + EPS)


def _split2(x):
    hi = x.astype(BF16)
    lo = (x - hi.astype(F32)).astype(BF16)
    return hi, lo


def _split3(x):
    a = x.astype(BF16)
    r = x - a.astype(F32)
    b = r.astype(BF16)
    c = (r - b.astype(F32)).astype(BF16)
    return a, b, c


def _const_spec(shape):
    nd = len(shape)
    return pl.BlockSpec(shape, lambda *_: (0,) * nd, pipeline_mode=pl.Buffered(1))


def _ada_kernel(c_ref, w_ref, b_ref, o_ref):
    c = c_ref[...]
    s = c * jax.nn.sigmoid(c)
    o_ref[...] = jnp.dot(s, w_ref[...], precision=lax.Precision.HIGHEST,
                         preferred_element_type=F32) + b_ref[...]


def _ada(c, w_ada, b_ada):
    n, d = c.shape
    dn = w_ada.shape[1]
    tn = 1024
    return pl.pallas_call(
        _ada_kernel,
        grid=(dn // tn,),
        in_specs=[pl.BlockSpec((n, d), lambda j: (0, 0)),
                  pl.BlockSpec((d, tn), lambda j: (0, j)),
                  pl.BlockSpec((1, tn), lambda j: (0, j))],
        out_specs=pl.BlockSpec((n, tn), lambda j: (0, j)),
        out_shape=jax.ShapeDtypeStruct((n, dn), F32),
        compiler_params=_cparams(("arbitrary",)),
        name="ada",
    )(c, w_ada, b_ada.reshape(1, dn))


def _inproj_kernel(x_ref, sh_ref, sc_ref, wq_ref, wk_ref, wv_ref, wr_ref, wu_ref, wg_ref,
                   w2h_ref, w2l_ref, b2_ref,
                   q_ref, k_ref, v_ref, r_ref, u_ref, lg_ref, *, q_scale):
    x = x_ref[0]
    h = (_ln(x) * (1.0 + sc_ref[0]) + sh_ref[0]).astype(BF16)
    q_ref[0] = (_dot(h, wq_ref[...]) * q_scale).astype(BF16)
    k_ref[0] = _dot(h, wk_ref[...]).astype(BF16)
    v_ref[0] = _dot(h, wv_ref[...]).astype(BF16)
    r_ref[0] = _dot(h, wr_ref[...]).astype(BF16)
    u_ref[0] = _dot(h, wu_ref[...])
    glr = _dot(h, wg_ref[...])
    g_hi, g_lo = _split2(glr)
    z = (_dot(g_hi, w2h_ref[...]) + _dot(g_lo, w2h_ref[...]) + _dot(g_hi, w2l_ref[...])
         + b2_ref[...])
    lg_ref[0] = (jnp.minimum(z, 0.0) - jnp.log1p(jnp.exp(-jnp.abs(z)))) * (1.0 / GATE_NORM)


def _inproj(x, shift, scale, wts, tm, q_scale):
    b, s, d = x.shape
    wq, wk, wv, wr, wu, wg, w2h, w2l, b2 = wts
    kw, vw, pw = wq.shape[1], wv.shape[1], wu.shape[1]
    mrows = shift.shape[1]
    mblk = 1 if mrows == 1 else tm
    mod_spec = pl.BlockSpec((1, mblk, d), (lambda bi, i: (bi, 0, 0)) if mrows == 1
                            else (lambda bi, i: (bi, i, 0)))
    row = lambda w: pl.BlockSpec((1, tm, w), lambda bi, i: (bi, i, 0))
    return pl.pallas_call(
        functools.partial(_inproj_kernel, q_scale=q_scale),
        grid=(b, s // tm),
        in_specs=[row(d), mod_spec, mod_spec] + [_const_spec(w.shape) for w in wts],
        out_specs=[row(kw), row(kw), row(vw), row(vw), row(pw), row(kw)],
        out_shape=[jax.ShapeDtypeStruct((b, s, kw), BF16), jax.ShapeDtypeStruct((b, s, kw), BF16),
                   jax.ShapeDtypeStruct((b, s, vw), BF16), jax.ShapeDtypeStruct((b, s, vw), BF16),
                   jax.ShapeDtypeStruct((b, s, pw), F32), jax.ShapeDtypeStruct((b, s, kw), F32)],
        compiler_params=_cparams(("arbitrary", "arbitrary")),
        name="inproj",
    )(x, shift, scale, *wts)


def _bcast_rows(x, rows, sub):
    w = x.shape[1]
    parts = [jnp.zeros((sub, w), x.dtype) if r is None else jnp.broadcast_to(x[r:r + 1, :], (sub, w))
             for r in rows]
    return parts[0] if len(parts) == 1 else jnp.concatenate(parts, axis=0)


def _gla_head(q, k, v, G, st, e_mat, L):
    sub = GLA_SUB
    nb = L // sub
    row_blk = lax.broadcasted_iota(jnp.int32, (L, L), 0) // sub
    col_blk = lax.broadcasted_iota(jnp.int32, (L, L), 1) // sub
    t_loc = lax.broadcasted_iota(jnp.int32, (L, q.shape[1]), 0) % sub
    pieces = []
    for s_loc in range(sub):
        rows = [b * sub + s_loc for b in range(nb)]
        gs = _bcast_rows(G, rows, sub)
        ks = _bcast_rows(k, rows, sub)
        a = jnp.where(t_loc >= s_loc, q * ks * jnp.exp(G - gs), 0.0)
        pieces.append(a.astype(BF16))
    p = _dot(jnp.concatenate(pieces, axis=1), e_mat)
    if nb > 1:
        p = jnp.where(row_blk == col_blk, p, 0.0)
    if nb == 4:
        bs = [None] + [b * sub - 1 for b in range(1, nb)]
        be = [b * sub + sub - 1 for b in range(nb)]
        q_b = q * jnp.exp(G - _bcast_rows(G, bs, sub))
        k_b = k * jnp.exp(_bcast_rows(G, be, sub) - G)
        q3 = q[3 * sub:] * jnp.exp(G[3 * sub:] - _bcast_rows(G, [bs[2]], sub))
        k0 = k[:sub] * jnp.exp(_bcast_rows(G, [be[1]], sub) - G[:sub])
        q_a = jnp.concatenate([q_b[:3 * sub], q3], axis=0)
        k_a = jnp.concatenate([k0, k_b[sub:]], axis=0)
        nt = (((1,), (1,)), ((), ()))
        p_a = lax.dot_general(q_a.astype(BF16), k_a.astype(BF16), nt, preferred_element_type=F32)
        p_b = lax.dot_general(q_b.astype(BF16), k_b.astype(BF16), nt, preferred_element_type=F32)
        in_a = jnp.where(row_blk >= 2, jnp.where(col_blk < 2, 1.0, 0.0), 0.0)
        in_b = jnp.where(row_blk - col_blk == 1, jnp.where(row_blk % 2 == 1, 1.0, 0.0), 0.0)
        p = p + p_a * in_a + p_b * in_b
    elif nb != 1:
        raise NotImplementedError("GLA chunk must be 1 or 4 sub-blocks")
    o = _dot(p.astype(BF16), v)
    qt = (q * jnp.exp(G)).astype(BF16)
    o = o + lax.dot_general(qt, st.astype(BF16), (((1,), (1,)), ((), ())), preferred_element_type=F32)
    g_last = G[L - 1:L, :]
    kt = (k * jnp.exp(g_last - G)).astype(BF16)
    st_new = st * jnp.exp(g_last) + lax.dot_general(v, kt, (((0,), (0,)), ((), ())),
                                                    preferred_element_type=F32)
    return o, st_new


def _gla_kernel(*refs, L, heads, dk, dv, has_init):
    if has_init:
        (q_ref, k_ref, v_ref, lg_ref, r_ref, g_ref, tri_ref, e_ref, s0_ref,
         og_ref, sout_ref, st_ref) = refs
    else:
        (q_ref, k_ref, v_ref, lg_ref, r_ref, g_ref, tri_ref, e_ref,
         og_ref, sout_ref, st_ref) = refs
        s0_ref = None
    j = pl.program_id(1)
    tq = q_ref.shape[1]

    @pl.when(j == 0)
    def _():
        for h in range(heads):
            if has_init:
                st_ref[h] = s0_ref[0, h].T
            else:
                st_ref[h] = jnp.zeros((dv, dk), F32)

    def chunk(c, carry):
        r0 = pl.multiple_of(c * L, L)
        rows = pl.ds(r0, L)
        lg = lg_ref[0, rows, :]
        tri = tri_ref[...]
        G_all = sum(_dot(tri, part) for part in _split3(lg))
        q_all = q_ref[0, rows, :].astype(F32)
        k_all = k_ref[0, rows, :].astype(F32)
        v_all = v_ref[0, rows, :]
        r_all = r_ref[0, rows, :].astype(F32)
        for h in range(heads):
            ks = slice(h * dk, (h + 1) * dk)
            vs = slice(h * dv, (h + 1) * dv)
            o, st_new = _gla_head(q_all[:, ks], k_all[:, ks], v_all[:, vs], G_all[:, ks],
                                  st_ref[h], e_ref[...], L)
            st_ref[h] = st_new
            on = o * lax.rsqrt(jnp.mean(o * o, -1, keepdims=True) + EPS)
            rr = r_all[:, vs]
            og_ref[0, rows, vs] = (on * g_ref[:, vs] * (rr * jax.nn.sigmoid(rr))).astype(BF16)
        return carry

    lax.fori_loop(0, tq // L, chunk, 0)

    @pl.when(j == pl.num_programs(1) - 1)
    def _():
        for h in range(heads):
            sout_ref[0, h] = st_ref[h].T


def _gla(q, k, v, lg, r, g, s0, L, tq, heads):
    b, s, kw = q.shape
    vw = v.shape[2]
    dk, dv = kw // heads, vw // heads
    sub = GLA_SUB
    tri = (jnp.arange(L)[:, None] >= jnp.arange(L)[None, :]).astype(BF16)
    e_mat = ((jnp.arange(sub * dk)[:, None] // dk) == (jnp.arange(L)[None, :] % sub)).astype(BF16)
    row = lambda w: pl.BlockSpec((1, tq, w), lambda bi, i: (bi, i, 0))
    st_spec = pl.BlockSpec((1, heads, dk, dv), lambda bi, i: (bi, 0, 0, 0))
    has_init = s0 is not None
    in_specs = [row(kw), row(kw), row(vw), row(kw), row(vw), _const_spec((1, vw)),
                _const_spec(tri.shape), _const_spec(e_mat.shape)]
    args = [q, k, v, lg, r, g.reshape(1, vw), tri, e_mat]
    if has_init:
        in_specs.append(st_spec)
        args.append(s0)
    return pl.pallas_call(
        functools.partial(_gla_kernel, L=L, heads=heads, dk=dk, dv=dv, has_init=has_init),
        grid=(b, s // tq),
        in_specs=in_specs,
        out_specs=[row(vw), st_spec],
        out_shape=[jax.ShapeDtypeStruct((b, s, vw), BF16),
                   jax.ShapeDtypeStruct((b, heads, dk, dv), F32)],
        scratch_shapes=[pltpu.VMEM((heads, dv, dk), F32)],
        compiler_params=_cparams(("arbitrary", "arbitrary")),
        name="gla",
    )(*args)


def _pool_kernel(u_ref, hist_ref, wp_ref, ps_ref, y_ref, so_ref, ext_ref, *, pos0):
    j = pl.program_id(1)
    tp = u_ref.shape[1]
    pad = POOL_HIST + 1
    gc = wp_ref.shape[1]

    @pl.when(j == 0)
    def _():
        ext_ref[0:pad, :] = hist_ref[0]

    @pl.when(j > 0)
    def _():
        ext_ref[0:pad, :] = ext_ref[tp:tp + pad, :]

    ext_ref[pad:pad + tp, :] = u_ref[0]
    pos = pos0 + j * tp + lax.broadcasted_iota(jnp.int32, (tp, gc), 0)
    for gi, w in enumerate(POOL_WINDOWS):
        cols = slice(gi * gc, (gi + 1) * gc)
        cur = ext_ref[pad:pad + tp, cols]
        win = cur
        for back in range(1, w):
            win = win + ext_ref[pad - back:pad - back + tp, cols]
        cnt = jnp.minimum(pos + 1, w).astype(F32)
        dlt = win / cnt - cur
        y = _dot(dlt.astype(BF16), wp_ref[gi]) * ps_ref[:, cols]
        y_ref[0, :, cols] = y.astype(BF16)

    @pl.when(j == pl.num_programs(1) - 1)
    def _():
        so_ref[0] = ext_ref[tp + 1:tp + pad, :]


def _pool(u, hist, w_pool, pool_scale, pos0, tp):
    b, s, pw = u.shape
    pad = POOL_HIST + 1
    row = lambda dt: pl.BlockSpec((1, tp, pw), lambda bi, i: (bi, i, 0))
    return pl.pallas_call(
        functools.partial(_pool_kernel, pos0=pos0),
        grid=(b, s // tp),
        in_specs=[row(F32), pl.BlockSpec((1, pad, pw), lambda bi, i: (bi, 0, 0)),
                  _const_spec(w_pool.shape), _const_spec((1, pw))],
        out_specs=[row(BF16), pl.BlockSpec((1, POOL_HIST, pw), lambda bi, i: (bi, 0, 0))],
        out_shape=[jax.ShapeDtypeStruct((b, s, pw), BF16),
                   jax.ShapeDtypeStruct((b, POOL_HIST, pw), F32)],
        scratch_shapes=[pltpu.VMEM((pad + tp, pw), F32)],
        compiler_params=_cparams(("arbitrary", "arbitrary")),
        name="pool",
    )(u, hist, w_pool, pool_scale.reshape(1, pw))


def _outproj_kernel(*refs, alpha, aliased):
    (x_ref, og_ref, op_ref, gt_ref, shf_ref, scf_ref, wo_ref, g1_ref, b1_ref,
     rwh_ref, rwl_ref, rb_ref) = refs[:12]
    x1_ref, h2_ref, ti_ref, tg_ref = refs[-4:]
    gw = og_ref.shape[2]
    mix = _dot(og_ref[0], wo_ref[0:gw, :]) + _dot(op_ref[0], wo_ref[gw:, :])
    x1 = _ln(alpha * x_ref[0] + gt_ref[0] * mix) * g1_ref[...] + b1_ref[...]
    x1_ref[0] = x1
    h2 = _ln(x1) * (1.0 + scf_ref[0]) + shf_ref[0]
    h2_ref[...] = h2
    h_hi, h_lo = _split2(h2)
    logits = (_dot(h_hi, rwh_ref[...]) + _dot(h_lo, rwh_ref[...]) + _dot(h_hi, rwl_ref[...])
              + rb_ref[...])
    lane = lax.broadcasted_iota(jnp.int32, logits.shape, 1)
    vals, idxs = [], []
    l = logits
    for _ in range(TOP_K):
        m = jnp.max(l, -1, keepdims=True)
        idx = jnp.min(jnp.where(l == m, lane, LANES), -1, keepdims=True)
        vals.append(m)
        idxs.append(idx)
        l = jnp.where(lane == idx, -jnp.inf, l)
    es = [jnp.exp(vv - vals[0]) for vv in vals]
    den = es[0]
    for e in es[1:]:
        den = den + e
    ti = jnp.zeros(logits.shape, jnp.int32)
    tg = jnp.zeros(logits.shape, F32)
    for kk in range(TOP_K):
        ti = jnp.where(lane == kk, idxs[kk], ti)
        tg = jnp.where(lane == kk, es[kk] / den, tg)
    ti_ref[...] = ti
    tg_ref[...] = tg


def _outproj(x, og, op, gt, shf, scf, w_out, ln_g, ln_b, rwh, rwl, rb, alpha, tm,
             t_total, row_off, prev):
    b, s, d = x.shape
    gw = og.shape[2]
    nblk = s // tm
    mrows = gt.shape[1]
    mblk = 1 if mrows == 1 else tm
    mod_spec = pl.BlockSpec((1, mblk, d), (lambda bi, i: (bi, 0, 0)) if mrows == 1
                            else (lambda bi, i: (bi, i, 0)))
    row = lambda w: pl.BlockSpec((1, tm, w), lambda bi, i: (bi, i, 0))
    off = row_off // tm
    flat = lambda w: pl.BlockSpec((tm, w), lambda bi, i: (off + bi * nblk + i, 0))
    in_specs = [row(d), row(gw), row(gw), mod_spec, mod_spec, mod_spec, _const_spec(w_out.shape),
                _const_spec((1, d)), _const_spec((1, d)), _const_spec(rwh.shape),
                _const_spec(rwl.shape), _const_spec((1, LANES))]
    args = [x, og, op, gt, shf, scf, w_out, ln_g.reshape(1, d), ln_b.reshape(1, d), rwh, rwl, rb]
    aliases = {}
    if prev is not None:
        for n, buf in enumerate(prev):
            in_specs.append(pl.BlockSpec(memory_space=pl.ANY))
            aliases[len(args)] = 1 + n
            args.append(buf)
    return pl.pallas_call(
        functools.partial(_outproj_kernel, alpha=alpha, aliased=prev is not None),
        grid=(b, nblk),
        in_specs=in_specs,
        out_specs=[row(d), flat(d), flat(LANES), flat(LANES)],
        out_shape=[jax.ShapeDtypeStruct((b, s, d), F32),
                   jax.ShapeDtypeStruct((t_total, d), F32),
                   jax.ShapeDtypeStruct((t_total, LANES), jnp.int32),
                   jax.ShapeDtypeStruct((t_total, LANES), F32)],
        input_output_aliases=aliases,
        compiler_params=_cparams(("arbitrary", "arbitrary")),
        name="outproj",
    )(*args)


def _moe_kernel(be_ref, nu_ref, src_ref, srcn_ref, dst_ref, h_hbm,
                wg_ref, bg_ref, wu_ref, bu_ref, wd_ref, bd_ref, y_hbm,
                xbuf, xb, acc, ybuf, gsem, ssem):
    i = pl.program_id(0)
    f = pl.program_id(1)
    nf = pl.num_programs(1)
    n_used = nu_ref[0]
    blk = xb.shape[0]
    slot = i % 2

    def gather(idx_ref, s):
        def body(r, c):
            t = idx_ref[0, 0, r]
            pltpu.make_async_copy(h_hbm.at[pl.ds(t, 1)], xbuf.at[s, pl.ds(r, 1)], gsem.at[s]).start()
            return c
        lax.fori_loop(0, blk, body, 0)

    def gather_wait(s):
        pltpu.make_async_copy(h_hbm.at[pl.ds(0, blk)], xbuf.at[s], gsem.at[s]).wait()

    def scatter(s):
        def body(r, c):
            t = dst_ref[0, 0, r]
            pltpu.make_async_copy(ybuf.at[s, pl.ds(r, 1)], y_hbm.at[pl.ds(t, 1)], ssem.at[s]).start()
            return c
        lax.fori_loop(0, blk, body, 0)

    def scatter_wait(s):
        pltpu.make_async_copy(ybuf.at[s], y_hbm.at[pl.ds(0, blk)], ssem.at[s]).wait()

    @pl.when(i < n_used)
    def _():
        @pl.when(f == 0)
        def _():
            @pl.when(i == 0)
            def _():
                gather(src_ref, 0)

            gather_wait(slot)
            xb[...] = xbuf[slot].astype(BF16)

            @pl.when(i + 1 < n_used)
            def _():
                gather(srcn_ref, 1 - slot)

        x = xb[...]
        g = jnp.minimum(_dot(x, wg_ref[0]) + bg_ref[0], SWIGLU_LIMIT)
        u = jnp.clip(_dot(x, wu_ref[0]) + bu_ref[0], -SWIGLU_LIMIT, SWIGLU_LIMIT)
        a = (g * jax.nn.sigmoid(SWIGLU_ALPHA * g) * (u + 1.0)).astype(BF16)
        part = _dot(a, wd_ref[0])

        @pl.when(f == 0)
        def _():
            acc[...] = part

        @pl.when(f > 0)
        def _():
            acc[...] += part

        @pl.when(f == nf - 1)
        def _():
            ybuf[slot] = acc[...] + bd_ref[0]
            scatter(slot)

            @pl.when(i > 0)
            def _():
                scatter_wait(1 - slot)

            @pl.when(i == n_used - 1)
            def _():
                scatter_wait(slot)


def _moe(h2, block_e, n_used, src, dst, wg, bg, wu, bu, wd, bd):
    t_total, d = h2.shape
    ne, _, dff = wg.shape
    n_blocks = block_e.shape[0]
    blk, tf = MOE_BLK, MOE_TF
    nf = dff // tf
    src3 = src.reshape(n_blocks, 1, blk)
    dst3 = dst.reshape(n_blocks, 1, blk)

    def fi(i, f, be, nu):
        return jnp.where(i < nu[0], f, nf - 1)

    idx_spec = lambda imap: pl.BlockSpec((1, 1, blk), imap, memory_space=pltpu.SMEM)
    grid_spec = pltpu.PrefetchScalarGridSpec(
        num_scalar_prefetch=2,
        grid=(n_blocks, nf),
        in_specs=[
            idx_spec(lambda i, f, be, nu: (i, 0, 0)),
            idx_spec(lambda i, f, be, nu: (jnp.minimum(i + 1, n_blocks - 1), 0, 0)),
            idx_spec(lambda i, f, be, nu: (i, 0, 0)),
            pl.BlockSpec(memory_space=pl.ANY),
            pl.BlockSpec((1, d, tf), lambda i, f, be, nu: (be[i], 0, fi(i, f, be, nu))),
            pl.BlockSpec((1, 1, tf), lambda i, f, be, nu: (be[i], 0, fi(i, f, be, nu))),
            pl.BlockSpec((1, d, tf), lambda i, f, be, nu: (be[i], 0, fi(i, f, be, nu))),
            pl.BlockSpec((1, 1, tf), lambda i, f, be, nu: (be[i], 0, fi(i, f, be, nu))),
            pl.BlockSpec((1, tf, d), lambda i, f, be, nu: (be[i], fi(i, f, be, nu), 0)),
            pl.BlockSpec((1, 1, d), lambda i, f, be, nu: (be[i], 0, 0)),
        ],
        out_specs=pl.BlockSpec(memory_space=pl.ANY),
        scratch_shapes=[pltpu.VMEM((2, blk, d), F32), pltpu.VMEM((blk, d), BF16),
                        pltpu.VMEM((blk, d), F32), pltpu.VMEM((2, blk, d), F32),
                        pltpu.SemaphoreType.DMA((2,)), pltpu.SemaphoreType.DMA((2,))],
    )
    return pl.pallas_call(
        _moe_kernel,
        grid_spec=grid_spec,
        out_shape=jax.ShapeDtypeStruct((n_blocks * blk, d), F32),
        compiler_params=_cparams(("arbitrary", "arbitrary")),
        name="moe",
    )(block_e, n_used, src3, src3, dst3, h2, wg, bg.reshape(ne, 1, dff), wu, bu.reshape(ne, 1, dff),
      wd, bd.reshape(ne, 1, d))


def _route(topi, blk, n_experts):
    t_total = topi.shape[0]
    n_assign = t_total * TOP_K
    flat_e = topi[:, :TOP_K].reshape(-1)
    order = jnp.argsort(flat_e).astype(jnp.int32)
    sorted_e = flat_e[order]
    counts = jnp.bincount(flat_e, length=n_experts).astype(jnp.int32)
    padded = (counts + blk - 1) // blk * blk
    start = jnp.cumsum(counts) - counts
    pad_end = jnp.cumsum(padded)
    pad_start = pad_end - padded
    slot = pad_start[sorted_e] + jnp.arange(n_assign, dtype=jnp.int32) - start[sorted_e]
    n_blocks = -(-n_assign // blk) + n_experts
    n_slots = n_blocks * blk
    inv = jnp.full((n_slots,), -1, jnp.int32).at[slot].set(order)
    valid = inv >= 0
    src = jnp.where(valid, inv // TOP_K, 0).astype(jnp.int32)
    pad_rank = jnp.cumsum(jnp.logical_not(valid).astype(jnp.int32)) - 1
    dst = jnp.where(valid, inv, n_assign + pad_rank).astype(jnp.int32)
    n_used = (pad_end[-1] // blk).astype(jnp.int32)
    blk_start = jnp.minimum(jnp.arange(n_blocks, dtype=jnp.int32), n_used - 1) * blk
    block_e = jnp.minimum(jnp.searchsorted(pad_end, blk_start, side='right'),
                          n_experts - 1).astype(jnp.int32)
    return block_e, n_used.reshape(1), src, dst


def _final_kernel(x1_ref, y4_ref, tg_ref, gt_ref, g2_ref, b2_ref, o_ref, *, alpha):
    d = x1_ref.shape[2]
    tg = tg_ref[...]
    f = tg[:, 0:1] * y4_ref[:, 0:d]
    for kk in range(1, TOP_K):
        f = f + tg[:, kk:kk + 1] * y4_ref[:, kk * d:(kk + 1) * d]
    o_ref[0] = _ln(alpha * x1_ref[0] + gt_ref[0] * f) * g2_ref[...] + b2_ref[...]


def _final(x1, y4, tg, gt, ln_g, ln_b, alpha, tm, row_off):
    b, s, d = x1.shape
    nblk = s // tm
    off = row_off // tm
    mrows = gt.shape[1]
    mblk = 1 if mrows == 1 else tm
    mod_spec = pl.BlockSpec((1, mblk, d), (lambda bi, i: (bi, 0, 0)) if mrows == 1
                            else (lambda bi, i: (bi, i, 0)))
    row = pl.BlockSpec((1, tm, d), lambda bi, i: (bi, i, 0))
    flat = lambda w: pl.BlockSpec((tm, w), lambda bi, i: (off + bi * nblk + i, 0))
    return pl.pallas_call(
        functools.partial(_final_kernel, alpha=alpha),
        grid=(b, nblk),
        in_specs=[row, flat(TOP_K * d), flat(LANES), mod_spec, _const_spec((1, d)),
                  _const_spec((1, d))],
        out_specs=row,
        out_shape=jax.ShapeDtypeStruct((b, s, d), F32),
        compiler_params=_cparams(("arbitrary", "arbitrary")),
        name="final",
    )(x1, y4, tg, gt, ln_g.reshape(1, d), ln_b.reshape(1, d))


def _layer(xp, xs, cp, cs, gla_s, pool_c, w_in, w_gate_lr, b_gate_lr, gla_norm_g, w_pool,
           pool_scale, w_out, w_ada, b_ada, ln1_g, ln1_b, ln2_g, ln2_b, router_w, router_b,
           w_gate, b_gate, w_up, b_up, w_down, b_down, alpha):
    bp, sp, d = xp.shape
    bs, ss, _ = xs.shape
    heads = gla_s.shape[1]
    dk, dv = gla_s.shape[2], gla_s.shape[3]
    kw, vw = heads * dk, heads * dv
    rank = w_gate_lr.shape[0]
    pw = pool_c.shape[2]
    n_experts = router_w.shape[1]
    assert dk == LANES and rank <= LANES and n_experts <= LANES

    n_c = bp + bs
    n_pad = -(-n_c // 8) * 8
    c_all = jnp.concatenate([cp, cs, jnp.zeros((n_pad - n_c, d), F32)], axis=0)
    mod = _ada(c_all, w_ada, b_ada)
    mods_p = [mod[:bp, i * d:(i + 1) * d].reshape(bp, 1, d) for i in range(6)]
    ns = bs * ss
    mods_s = [jnp.repeat(mod[bp:n_c, i * d:(i + 1) * d], ss, axis=0).reshape(1, ns, d)
              for i in range(6)]

    o = 2 * kw + 2 * vw
    w2 = jnp.zeros((LANES, kw), F32).at[:rank].set(w_gate_lr)
    w2h, w2l = _split2(w2)
    wglr = jnp.zeros((d, LANES), F32).at[:, :rank].set(w_in[:, o:o + rank]).astype(BF16)
    wts = (w_in[:, :kw].astype(BF16), w_in[:, kw:2 * kw].astype(BF16),
           w_in[:, 2 * kw:2 * kw + vw].astype(BF16), w_in[:, 2 * kw + vw:o].astype(BF16),
           w_in[:, o + rank:].astype(BF16), wglr, w2h, w2l, b_gate_lr.reshape(1, kw))
    w_out_b = w_out.astype(BF16)
    w_pool_b = w_pool.astype(BF16)
    rw = jnp.zeros((d, LANES), F32).at[:, :n_experts].set(router_w)
    rwh, rwl = _split2(rw)
    rb = jnp.full((1, LANES), NEG, F32).at[0, :n_experts].set(router_b)

    t_total = bp * sp + ns
    tm_p = min(512, sp)

    q_scale = float(dk) ** -0.5
    q, k, v, r, u, lg = _inproj(xp, mods_p[0], mods_p[1], wts, tm_p, q_scale)
    og, gla_p = _gla(q, k, v, lg, r, gla_norm_g, None, GLA_CHUNK, tm_p, heads)
    hist0 = jnp.zeros((bp, POOL_HIST + 1, pw), F32)
    o_pool, pool_p = _pool(u, hist0, w_pool_b, pool_scale, 0, tm_p)
    x1p, h2, topi, topg = _outproj(xp, og, o_pool, mods_p[2], mods_p[3], mods_p[4], w_out_b,
                                   ln1_g, ln1_b, rwh, rwl, rb, alpha, tm_p, t_total, 0, None)

    xs_flat = xs.reshape(1, ns, d)
    q, k, v, r, u, lg = _inproj(xs_flat, mods_s[0], mods_s[1], wts, ns, q_scale)
    seq = lambda a: a.reshape(bs, ss, a.shape[-1])
    og, gla_s_new = _gla(seq(q), seq(k), seq(v), seq(lg), seq(r), gla_norm_g, gla_s, ss, ss, heads)
    hist_s = jnp.concatenate([jnp.zeros((bs, 1, pw), F32), pool_c], axis=1)
    o_pool, pool_s_new = _pool(seq(u), hist_s, w_pool_b, pool_scale, PAST_LEN, ss)
    x1s, h2, topi, topg = _outproj(xs_flat, og.reshape(1, ns, vw), o_pool.reshape(1, ns, pw),
                                   mods_s[2], mods_s[3], mods_s[4], w_out_b, ln1_g, ln1_b,
                                   rwh, rwl, rb, alpha, ns, t_total, bp * sp, (h2, topi, topg))

    block_e, n_used, src, dst = _route(topi, MOE_BLK, n_experts)
    y4 = _moe(h2, block_e, n_used, src, dst, w_gate.astype(BF16), b_gate, w_up.astype(BF16), b_up,
              w_down.astype(BF16), b_down)
    y4 = y4.reshape(y4.shape[0] // TOP_K, TOP_K * d)
    tm_f = min(256, sp)
    yp = _final(x1p, y4, topg, mods_p[5], ln2_g, ln2_b, alpha, tm_f, 0)
    ys = _final(x1s, y4, topg, mods_s[5], ln2_g, ln2_b, alpha, ns, bp * sp)
    return yp, ys.reshape(bs, ss, d), gla_p, gla_s_new, pool_p, pool_s_new


def kernel(x_prompt, x_sample, c_prompt, c_sample, state_gla, cache_pool, w_in, w_gate_lr, b_gate_lr, gla_norm_g, w_pool, pool_scale, w_out, w_ada, b_ada, ln1_g, ln1_b, ln2_g, ln2_b, router_w, router_b, w_gate, b_gate, w_up, b_up, w_down, b_down):
    depth = w_in.shape[0]
    alpha = float((2 * depth) ** 0.25)
    xp, xs = x_prompt, x_sample
    gla_p_all, gla_s_all, pool_p_all, pool_s_all = [], [], [], []
    for l in range(depth):
        xp, xs, gla_p, gla_s, pool_p, pool_s = _layer(
            xp, xs, c_prompt, c_sample, state_gla[l], cache_pool[l], w_in[l], w_gate_lr[l],
            b_gate_lr[l], gla_norm_g[l], w_pool[l], pool_scale[l], w_out[l], w_ada[l], b_ada[l],
            ln1_g[l], ln1_b[l], ln2_g[l], ln2_b[l], router_w[l], router_b[l], w_gate[l], b_gate[l],
            w_up[l], b_up[l], w_down[l], b_down[l], alpha)
        gla_p_all.append(gla_p)
        gla_s_all.append(gla_s)
        pool_p_all.append(pool_p)
        pool_s_all.append(pool_s)
    return (xp, xs, jnp.stack(gla_p_all, 0), jnp.stack(gla_s_all, 0),
            jnp.stack(pool_p_all, 0), jnp.stack(pool_s_all, 0))
```

```python
import functools

import jax
import jax.numpy as jnp
from jax import lax
from jax.experimental import pallas as pl
from jax.experimental.pallas import tpu as pltpu

F32 = jnp.float32
BF16 = jnp.bfloat16

EPS = 1e-5
GLA_CHUNK = 64
GLA_SUB = 16
GATE_NORM = 16.0
POOL_WINDOWS = (2, 4, 8, 16)
POOL_HIST = max(POOL_WINDOWS) - 1
PAST_LEN = 1024
TOP_K = 4
SWIGLU_LIMIT = 7.0
SWIGLU_ALPHA = 1.702
LANES = 128
MOE_BLK = 512
MOE_TF = 512
MOE_SPREAD = 0
VMEM_LIMIT = 56 * 1024 * 1024
NEG = -1e30


def _cparams(sem):
    return pltpu.CompilerParams(dimension_semantics=sem, vmem_limit_bytes=VMEM_LIMIT)


def _dot(a, b):
    return jnp.dot(a, b, preferred_element_type=F32)


def _ln(x):
    mu = jnp.mean(x, -1, keepdims=True)
    xc = x - mu
    var = jnp.mean(xc * xc, -1, keepdims=True)
    return xc * lax.rsqrt(var + EPS)


def _split2(x):
    hi = x.astype(BF16)
    lo = (x - hi.astype(F32)).astype(BF16)
    return hi, lo


def _split3(x):
    a = x.astype(BF16)
    r = x - a.astype(F32)
    b = r.astype(BF16)
    c = (r - b.astype(F32)).astype(BF16)
    return a, b, c


def _const_spec(shape):
    nd = len(shape)
    return pl.BlockSpec(shape, lambda *_: (0,) * nd, pipeline_mode=pl.Buffered(1))


def _ada_kernel(c_ref, w_ref, b_ref, o_ref):
    c = c_ref[...]
    s = c * jax.nn.sigmoid(c)
    o_ref[...] = jnp.dot(s, w_ref[...], precision=lax.Precision.HIGHEST,
                         preferred_element_type=F32) + b_ref[...]


def _ada(c, w_ada, b_ada):
    n, d = c.shape
    dn = w_ada.shape[1]
    tn = 1024
    return pl.pallas_call(
        _ada_kernel,
        grid=(dn // tn,),
        in_specs=[pl.BlockSpec((n, d), lambda j: (0, 0)),
                  pl.BlockSpec((d, tn), lambda j: (0, j)),
                  pl.BlockSpec((1, tn), lambda j: (0, j))],
        out_specs=pl.BlockSpec((n, tn), lambda j: (0, j)),
        out_shape=jax.ShapeDtypeStruct((n, dn), F32),
        compiler_params=_cparams(("arbitrary",)),
        name="ada",
    )(c, w_ada, b_ada.reshape(1, dn))


def _inproj_kernel(x_ref, sh_ref, sc_ref, wq_ref, wk_ref, wv_ref, wr_ref, wu_ref, wg_ref,
                   w2h_ref, w2l_ref, b2_ref,
                   q_ref, k_ref, v_ref, r_ref, u_ref, lg_ref, *, q_scale):
    x = x_ref[0]
    h = (_ln(x) * (1.0 + sc_ref[0]) + sh_ref[0]).astype(BF16)
    q_ref[0] = (_dot(h, wq_ref[...]) * q_scale).astype(BF16)
    k_ref[0] = _dot(h, wk_ref[...]).astype(BF16)
    v_ref[0] = _dot(h, wv_ref[...]).astype(BF16)
    r_ref[0] = _dot(h, wr_ref[...]).astype(BF16)
    u_ref[0] = _dot(h, wu_ref[...])
    glr = _dot(h, wg_ref[...])
    g_hi, g_lo = _split2(glr)
    z = (_dot(g_hi, w2h_ref[...]) + _dot(g_lo, w2h_ref[...]) + _dot(g_hi, w2l_ref[...])
         + b2_ref[...])
    lg_ref[0] = (jnp.minimum(z, 0.0) - jnp.log1p(jnp.exp(-jnp.abs(z)))) * (1.0 / GATE_NORM)


def _inproj(x, shift, scale, wts, tm, q_scale):
    b, s, d = x.shape
    wq, wk, wv, wr, wu, wg, w2h, w2l, b2 = wts
    kw, vw, pw = wq.shape[1], wv.shape[1], wu.shape[1]
    mrows = shift.shape[1]
    mblk = 1 if mrows == 1 else tm
    mod_spec = pl.BlockSpec((1, mblk, d), (lambda bi, i: (bi, 0, 0)) if mrows == 1
                            else (lambda bi, i: (bi, i, 0)))
    row = lambda w: pl.BlockSpec((1, tm, w), lambda bi, i: (bi, i, 0))
    return pl.pallas_call(
        functools.partial(_inproj_kernel, q_scale=q_scale),
        grid=(b, s // tm),
        in_specs=[row(d), mod_spec, mod_spec] + [_const_spec(w.shape) for w in wts],
        out_specs=[row(kw), row(kw), row(vw), row(vw), row(pw), row(kw)],
        out_shape=[jax.ShapeDtypeStruct((b, s, kw), BF16), jax.ShapeDtypeStruct((b, s, kw), BF16),
                   jax.ShapeDtypeStruct((b, s, vw), BF16), jax.ShapeDtypeStruct((b, s, vw), BF16),
                   jax.ShapeDtypeStruct((b, s, pw), F32), jax.ShapeDtypeStruct((b, s, kw), F32)],
        compiler_params=_cparams(("arbitrary", "arbitrary")),
        name="inproj",
    )(x, shift, scale, *wts)


def _bcast_rows(x, rows, sub):
    w = x.shape[1]
    parts = [jnp.zeros((sub, w), x.dtype) if r is None else jnp.broadcast_to(x[r:r + 1, :], (sub, w))
             for r in rows]
    return parts[0] if len(parts) == 1 else jnp.concatenate(parts, axis=0)


def _gla_head(q, k, v, G, st, e_mat, L):
    sub = GLA_SUB
    nb = L // sub
    row_blk = lax.broadcasted_iota(jnp.int32, (L, L), 0) // sub
    col_blk = lax.broadcasted_iota(jnp.int32, (L, L), 1) // sub
    t_loc = lax.broadcasted_iota(jnp.int32, (L, q.shape[1]), 0) % sub
    pieces = []
    for s_loc in range(sub):
        rows = [b * sub + s_loc for b in range(nb)]
        gs = _bcast_rows(G, rows, sub)
        ks = _bcast_rows(k, rows, sub)
        a = jnp.where(t_loc >= s_loc, q * ks * jnp.exp(G - gs), 0.0)
        pieces.append(a.astype(BF16))
    p = _dot(jnp.concatenate(pieces, axis=1), e_mat)
    if nb > 1:
        p = jnp.where(row_blk == col_blk, p, 0.0)
    if nb == 4:
        bs = [None] + [b * sub - 1 for b in range(1, nb)]
        be = [b * sub + sub - 1 for b in range(nb)]
        q_b = q * jnp.exp(G - _bcast_rows(G, bs, sub))
        k_b = k * jnp.exp(_bcast_rows(G, be, sub) - G)
        q3 = q[3 * sub:] * jnp.exp(G[3 * sub:] - _bcast_rows(G, [bs[2]], sub))
        k0 = k[:sub] * jnp.exp(_bcast_rows(G, [be[1]], sub) - G[:sub])
        q_a = jnp.concatenate([q_b[:3 * sub], q3], axis=0)
        k_a = jnp.concatenate([k0, k_b[sub:]], axis=0)
        nt = (((1,), (1,)), ((), ()))
        p_a = lax.dot_general(q_a.astype(BF16), k_a.astype(BF16), nt, preferred_element_type=F32)
        p_b = lax.dot_general(q_b.astype(BF16), k_b.astype(BF16), nt, preferred_element_type=F32)
        in_a = jnp.where(row_blk >= 2, jnp.where(col_blk < 2, 1.0, 0.0), 0.0)
        in_b = jnp.where(row_blk - col_blk == 1, jnp.where(row_blk % 2 == 1, 1.0, 0.0), 0.0)
        p = p + p_a * in_a + p_b * in_b
    elif nb != 1:
        raise NotImplementedError("GLA chunk must be 1 or 4 sub-blocks")
    o = _dot(p.astype(BF16), v)
    qt = (q * jnp.exp(G)).astype(BF16)
    o = o + lax.dot_general(qt, st.astype(BF16), (((1,), (1,)), ((), ())), preferred_element_type=F32)
    g_last = G[L - 1:L, :]
    kt = (k * jnp.exp(g_last - G)).astype(BF16)
    st_new = st * jnp.exp(g_last) + lax.dot_general(v, kt, (((0,), (0,)), ((), ())),
                                                    preferred_element_type=F32)
    return o, st_new


def _gla_kernel(*refs, L, heads, dk, dv, has_init):
    if has_init:
        (q_ref, k_ref, v_ref, lg_ref, r_ref, g_ref, tri_ref, e_ref, s0_ref,
         og_ref, sout_ref, st_ref) = refs
    else:
        (q_ref, k_ref, v_ref, lg_ref, r_ref, g_ref, tri_ref, e_ref,
         og_ref, sout_ref, st_ref) = refs
        s0_ref = None
    j = pl.program_id(1)
    tq = q_ref.shape[1]

    @pl.when(j == 0)
    def _():
        for h in range(heads):
            if has_init:
                st_ref[h] = s0_ref[0, h].T
            else:
                st_ref[h] = jnp.zeros((dv, dk), F32)

    def chunk(c, carry):
        r0 = pl.multiple_of(c * L, L)
        rows = pl.ds(r0, L)
        lg = lg_ref[0, rows, :]
        tri = tri_ref[...]
        G_all = sum(_dot(tri, part) for part in _split3(lg))
        q_all = q_ref[0, rows, :].astype(F32)
        k_all = k_ref[0, rows, :].astype(F32)
        v_all = v_ref[0, rows, :]
        r_all = r_ref[0, rows, :].astype(F32)
        for h in range(heads):
            ks = slice(h * dk, (h + 1) * dk)
            vs = slice(h * dv, (h + 1) * dv)
            o, st_new = _gla_head(q_all[:, ks], k_all[:, ks], v_all[:, vs], G_all[:, ks],
                                  st_ref[h], e_ref[...], L)
            st_ref[h] = st_new
            on = o * lax.rsqrt(jnp.mean(o * o, -1, keepdims=True) + EPS)
            rr = r_all[:, vs]
            og_ref[0, rows, vs] = (on * g_ref[:, vs] * (rr * jax.nn.sigmoid(rr))).astype(BF16)
        return carry

    lax.fori_loop(0, tq // L, chunk, 0)

    @pl.when(j == pl.num_programs(1) - 1)
    def _():
        for h in range(heads):
            sout_ref[0, h] = st_ref[h].T


def _gla(q, k, v, lg, r, g, s0, L, tq, heads):
    b, s, kw = q.shape
    vw = v.shape[2]
    dk, dv = kw // heads, vw // heads
    sub = GLA_SUB
    tri = (jnp.arange(L)[:, None] >= jnp.arange(L)[None, :]).astype(BF16)
    e_mat = ((jnp.arange(sub * dk)[:, None] // dk) == (jnp.arange(L)[None, :] % sub)).astype(BF16)
    row = lambda w: pl.BlockSpec((1, tq, w), lambda bi, i: (bi, i, 0))
    st_spec = pl.BlockSpec((1, heads, dk, dv), lambda bi, i: (bi, 0, 0, 0))
    has_init = s0 is not None
    in_specs = [row(kw), row(kw), row(vw), row(kw), row(vw), _const_spec((1, vw)),
                _const_spec(tri.shape), _const_spec(e_mat.shape)]
    args = [q, k, v, lg, r, g.reshape(1, vw), tri, e_mat]
    if has_init:
        in_specs.append(st_spec)
        args.append(s0)
    return pl.pallas_call(
        functools.partial(_gla_kernel, L=L, heads=heads, dk=dk, dv=dv, has_init=has_init),
        grid=(b, s // tq),
        in_specs=in_specs,
        out_specs=[row(vw), st_spec],
        out_shape=[jax.ShapeDtypeStruct((b, s, vw), BF16),
                   jax.ShapeDtypeStruct((b, heads, dk, dv), F32)],
        scratch_shapes=[pltpu.VMEM((heads, dv, dk), F32)],
        compiler_params=_cparams(("arbitrary", "arbitrary")),
        name="gla",
    )(*args)


def _pool_kernel(u_ref, hist_ref, wp_ref, ps_ref, y_ref, so_ref, ext_ref, *, pos0):
    j = pl.program_id(1)
    tp = u_ref.shape[1]
    pad = POOL_HIST + 1
    gc = wp_ref.shape[1]

    @pl.when(j == 0)
    def _():
        ext_ref[0:pad, :] = hist_ref[0]

    @pl.when(j > 0)
    def _():
        ext_ref[0:pad, :] = ext_ref[tp:tp + pad, :]

    ext_ref[pad:pad + tp, :] = u_ref[0]
    pos = pos0 + j * tp + lax.broadcasted_iota(jnp.int32, (tp, gc), 0)
    for gi, w in enumerate(POOL_WINDOWS):
        cols = slice(gi * gc, (gi + 1) * gc)
        cur = ext_ref[pad:pad + tp, cols]
        win = cur
        for back in range(1, w):
            win = win + ext_ref[pad - back:pad - back + tp, cols]
        cnt = jnp.minimum(pos + 1, w).astype(F32)
        dlt = win / cnt - cur
        y = _dot(dlt.astype(BF16), wp_ref[gi]) * ps_ref[:, cols]
        y_ref[0, :, cols] = y.astype(BF16)

    @pl.when(j == pl.num_programs(1) - 1)
    def _():
        so_ref[0] = ext_ref[tp + 1:tp + pad, :]


def _pool(u, hist, w_pool, pool_scale, pos0, tp):
    b, s, pw = u.shape
    pad = POOL_HIST + 1
    row = lambda dt: pl.BlockSpec((1, tp, pw), lambda bi, i: (bi, i, 0))
    return pl.pallas_call(
        functools.partial(_pool_kernel, pos0=pos0),
        grid=(b, s // tp),
        in_specs=[row(F32), pl.BlockSpec((1, pad, pw), lambda bi, i: (bi, 0, 0)),
                  _const_spec(w_pool.shape), _const_spec((1, pw))],
        out_specs=[row(BF16), pl.BlockSpec((1, POOL_HIST, pw), lambda bi, i: (bi, 0, 0))],
        out_shape=[jax.ShapeDtypeStruct((b, s, pw), BF16),
                   jax.ShapeDtypeStruct((b, POOL_HIST, pw), F32)],
        scratch_shapes=[pltpu.VMEM((pad + tp, pw), F32)],
        compiler_params=_cparams(("arbitrary", "arbitrary")),
        name="pool",
    )(u, hist, w_pool, pool_scale.reshape(1, pw))


def _outproj_kernel(*refs, alpha, aliased):
    (x_ref, og_ref, op_ref, gt_ref, shf_ref, scf_ref, wo_ref, g1_ref, b1_ref,
     rwh_ref, rwl_ref, rb_ref) = refs[:12]
    x1_ref, h2_ref, ti_ref, tg_ref = refs[-4:]
    gw = og_ref.shape[2]
    mix = _dot(og_ref[0], wo_ref[0:gw, :]) + _dot(op_ref[0], wo_ref[gw:, :])
    x1 = _ln(alpha * x_ref[0] + gt_ref[0] * mix) * g1_ref[...] + b1_ref[...]
    x1_ref[0] = x1
    h2 = _ln(x1) * (1.0 + scf_ref[0]) + shf_ref[0]
    h2_ref[...] = h2
    h_hi, h_lo = _split2(h2)
    logits = (_dot(h_hi, rwh_ref[...]) + _dot(h_lo, rwh_ref[...]) + _dot(h_hi, rwl_ref[...])
              + rb_ref[...])
    lane = lax.broadcasted_iota(jnp.int32, logits.shape, 1)
    vals, idxs = [], []
    l = logits
    for _ in range(TOP_K):
        m = jnp.max(l, -1, keepdims=True)
        idx = jnp.min(jnp.where(l == m, lane, LANES), -1, keepdims=True)
        vals.append(m)
        idxs.append(idx)
        l = jnp.where(lane == idx, -jnp.inf, l)
    es = [jnp.exp(vv - vals[0]) for vv in vals]
    den = es[0]
    for e in es[1:]:
        den = den + e
    ti = jnp.zeros(logits.shape, jnp.int32)
    tg = jnp.zeros(logits.shape, F32)
    for kk in range(TOP_K):
        ti = jnp.where(lane == kk, idxs[kk], ti)
        tg = jnp.where(lane == kk, es[kk] / den, tg)
    ti_ref[...] = ti
    tg_ref[...] = tg


def _outproj(x, og, op, gt, shf, scf, w_out, ln_g, ln_b, rwh, rwl, rb, alpha, tm,
             t_total, row_off, prev):
    b, s, d = x.shape
    gw = og.shape[2]
    nblk = s // tm
    mrows = gt.shape[1]
    mblk = 1 if mrows == 1 else tm
    mod_spec = pl.BlockSpec((1, mblk, d), (lambda bi, i: (bi, 0, 0)) if mrows == 1
                            else (lambda bi, i: (bi, i, 0)))
    row = lambda w: pl.BlockSpec((1, tm, w), lambda bi, i: (bi, i, 0))
    off = row_off // tm
    flat = lambda w: pl.BlockSpec((tm, w), lambda bi, i: (off + bi * nblk + i, 0))
    in_specs = [row(d), row(gw), row(gw), mod_spec, mod_spec, mod_spec, _const_spec(w_out.shape),
                _const_spec((1, d)), _const_spec((1, d)), _const_spec(rwh.shape),
                _const_spec(rwl.shape), _const_spec((1, LANES))]
    args = [x, og, op, gt, shf, scf, w_out, ln_g.reshape(1, d), ln_b.reshape(1, d), rwh, rwl, rb]
    aliases = {}
    if prev is not None:
        for n, buf in enumerate(prev):
            in_specs.append(pl.BlockSpec(memory_space=pl.ANY))
            aliases[len(args)] = 1 + n
            args.append(buf)
    return pl.pallas_call(
        functools.partial(_outproj_kernel, alpha=alpha, aliased=prev is not None),
        grid=(b, nblk),
        in_specs=in_specs,
        out_specs=[row(d), flat(d), flat(LANES), flat(LANES)],
        out_shape=[jax.ShapeDtypeStruct((b, s, d), F32),
                   jax.ShapeDtypeStruct((t_total, d), F32),
                   jax.ShapeDtypeStruct((t_total, LANES), jnp.int32),
                   jax.ShapeDtypeStruct((t_total, LANES), F32)],
        input_output_aliases=aliases,
        compiler_params=_cparams(("arbitrary", "arbitrary")),
        name="outproj",
    )(*args)


def _moe_kernel(be_ref, nu_ref, nv_ref, src_ref, srcn_ref, dstp_ref, dst_ref, h_hbm,
                wg_ref, bg_ref, wu_ref, bu_ref, wd_ref, bd_ref, y_hbm,
                xbuf, xb, acc, ybuf, gsem, ssem, *, nf):
    i = pl.program_id(0)
    f = pl.program_id(1)
    n_used = nu_ref[0]
    blk = xb.shape[0]
    rps = blk // nf
    slot = i % 2
    other = 1 - slot

    def row_gather(idx_ref, r, s):
        t = idx_ref[0, 0, r]
        return pltpu.make_async_copy(h_hbm.at[pl.ds(t, 1)], xbuf.at[s, pl.ds(r, 1)], gsem.at[s])

    def row_scatter(idx_ref, r, s):
        t = idx_ref[0, 0, r]
        return pltpu.make_async_copy(ybuf.at[s, pl.ds(r, 1)], y_hbm.at[pl.ds(t, 1)], ssem.at[s])

    def scatter_wait(s, n):
        p = blk
        while p >= 8:
            @pl.when((n & p) != 0)
            def _():
                pltpu.make_async_copy(ybuf.at[s, pl.ds(0, p)], y_hbm.at[pl.ds(0, p)], ssem.at[s]).wait()
            p //= 2
        for q in range(1, 8):
            @pl.when((n & 7) >= q)
            def _():
                pltpu.make_async_copy(ybuf.at[s, pl.ds(0, 1)], y_hbm.at[pl.ds(0, 1)], ssem.at[s]).wait()

    @pl.when(i < n_used)
    def _():
        @pl.when(f == 0)
        def _():
            @pl.when(i == 0)
            def _():
                def body(r, c):
                    row_gather(src_ref, r, 0).start()
                    return c
                lax.fori_loop(0, blk, body, 0)

            pltpu.make_async_copy(h_hbm.at[pl.ds(0, blk)], xbuf.at[slot], gsem.at[slot]).wait()
            xb[...] = xbuf[slot].astype(BF16)

        has_next = i + 1 < n_used
        n_prev = jnp.where(i >= 1, nv_ref[jnp.maximum(i - 1, 0)], 0)
        row0 = pl.multiple_of(f * rps, rps)
        carry = n_used
        for rr in range(rps):
            r = row0 + rr
            t_src = srcn_ref[0, 0, r + jnp.minimum(carry, 0)]
            t_dst = dstp_ref[0, 0, r + jnp.minimum(t_src, 0)]
            carry = t_dst
            for _ in range(MOE_SPREAD):
                carry = jnp.minimum(carry + 1, 0)

            @pl.when(has_next)
            def _():
                pltpu.make_async_copy(h_hbm.at[pl.ds(t_src, 1)], xbuf.at[other, pl.ds(r, 1)],
                                      gsem.at[other]).start()

            @pl.when(r < n_prev)
            def _():
                pltpu.make_async_copy(ybuf.at[other, pl.ds(r, 1)], y_hbm.at[pl.ds(t_dst, 1)],
                                      ssem.at[other]).start()

        x = xb[...]
        g = jnp.minimum(_dot(x, wg_ref[0]) + bg_ref[0], SWIGLU_LIMIT)
        u = jnp.clip(_dot(x, wu_ref[0]) + bu_ref[0], -SWIGLU_LIMIT, SWIGLU_LIMIT)
        a = (g * jax.nn.sigmoid(SWIGLU_ALPHA * g) * (u + 1.0)).astype(BF16)
        part = _dot(a, wd_ref[0])

        @pl.when(f == 0)
        def _():
            acc[...] = part

        @pl.when(jnp.logical_and(f > 0, f < nf - 1))
        def _():
            acc[...] += part

        @pl.when(f == nf - 1)
        def _():
            @pl.when(i >= 2)
            def _():
                scatter_wait(slot, nv_ref[jnp.maximum(i - 2, 0)])

            ybuf[slot] = acc[...] + part + bd_ref[0]

            @pl.when(i == n_used - 1)
            def _():
                @pl.when(i >= 1)
                def _():
                    scatter_wait(other, nv_ref[jnp.maximum(i - 1, 0)])

                n_cur = nv_ref[i]

                def body(r, c):
                    row_scatter(dst_ref, r, slot).start()
                    return c
                lax.fori_loop(0, n_cur, body, 0)
                scatter_wait(slot, n_cur)


def _moe(h2, block_e, n_used, n_valid, src, dst, wg, bg, wu, bu, wd, bd):
    t_total, d = h2.shape
    ne, _, dff = wg.shape
    n_blocks = block_e.shape[0]
    blk, tf = MOE_BLK, MOE_TF
    nf = dff // tf
    assert nf >= 2 and blk % nf == 0
    src3 = src.reshape(n_blocks, 1, blk)
    dst3 = dst.reshape(n_blocks, 1, blk)

    def fi(i, f, nu):
        return jnp.where(i < nu[0], f, nf - 1)

    idx_spec = lambda imap: pl.BlockSpec((1, 1, blk), imap, memory_space=pltpu.SMEM)
    grid_spec = pltpu.PrefetchScalarGridSpec(
        num_scalar_prefetch=3,
        grid=(n_blocks, nf),
        in_specs=[
            idx_spec(lambda i, f, be, nu, nv: (i, 0, 0)),
            idx_spec(lambda i, f, be, nu, nv: (jnp.minimum(i + 1, n_blocks - 1), 0, 0)),
            idx_spec(lambda i, f, be, nu, nv: (jnp.maximum(i - 1, 0), 0, 0)),
            idx_spec(lambda i, f, be, nu, nv: (i, 0, 0)),
            pl.BlockSpec(memory_space=pl.ANY),
            pl.BlockSpec((1, d, tf), lambda i, f, be, nu, nv: (be[i], 0, fi(i, f, nu))),
            pl.BlockSpec((1, 1, tf), lambda i, f, be, nu, nv: (be[i], 0, fi(i, f, nu))),
            pl.BlockSpec((1, d, tf), lambda i, f, be, nu, nv: (be[i], 0, fi(i, f, nu))),
            pl.BlockSpec((1, 1, tf), lambda i, f, be, nu, nv: (be[i], 0, fi(i, f, nu))),
            pl.BlockSpec((1, tf, d), lambda i, f, be, nu, nv: (be[i], fi(i, f, nu), 0)),
            pl.BlockSpec((1, 1, d), lambda i, f, be, nu, nv: (be[i], 0, 0)),
        ],
        out_specs=pl.BlockSpec(memory_space=pl.ANY),
        scratch_shapes=[pltpu.VMEM((2, blk, d), F32), pltpu.VMEM((blk, d), BF16),
                        pltpu.VMEM((blk, d), F32), pltpu.VMEM((2, blk, d), F32),
                        pltpu.SemaphoreType.DMA((2,)), pltpu.SemaphoreType.DMA((2,))],
    )
    return pl.pallas_call(
        functools.partial(_moe_kernel, nf=nf),
        grid_spec=grid_spec,
        out_shape=jax.ShapeDtypeStruct((TOP_K * t_total, d), F32),
        compiler_params=_cparams(("arbitrary", "arbitrary")),
        name="moe",
    )(block_e, n_used, n_valid, src3, src3, dst3, dst3, h2, wg, bg.reshape(ne, 1, dff), wu,
      bu.reshape(ne, 1, dff), wd, bd.reshape(ne, 1, d))


def _route(topi, blk, n_experts):
    t_total = topi.shape[0]
    n_assign = t_total * TOP_K
    shift = max(n_assign - 1, 1).bit_length()
    flat_e = topi[:, :TOP_K].reshape(-1)
    key = jnp.sort(flat_e * (1 << shift) + jnp.arange(n_assign, dtype=jnp.int32))
    order = key & ((1 << shift) - 1)
    experts = jnp.arange(n_experts, dtype=jnp.int32)
    counts = jnp.sum((flat_e[:, None] == experts[None, :]).astype(jnp.int32), axis=0)
    padded = (counts + blk - 1) // blk * blk
    start = jnp.cumsum(counts) - counts
    pad_end = jnp.cumsum(padded)
    pad_start = pad_end - padded
    n_blocks = -(-n_assign // blk) + n_experts
    n_used = pad_end[-1] // blk
    blk_row0 = jnp.minimum(jnp.arange(n_blocks, dtype=jnp.int32), n_used - 1) * blk
    block_e = jnp.minimum(jnp.sum((pad_end[None, :] <= blk_row0[:, None]).astype(jnp.int32), axis=1),
                          n_experts - 1)
    in_blk0 = blk_row0 - pad_start[block_e]
    n_valid = jnp.clip(counts[block_e] - in_blk0, 0, blk)
    j = in_blk0[:, None] + jnp.arange(blk, dtype=jnp.int32)[None, :]
    valid = j < counts[block_e][:, None]
    pos = jnp.clip(start[block_e][:, None] + j, 0, n_assign - 1)
    assign = jnp.where(valid, order[pos], 0)
    src = assign // TOP_K
    dst = (assign % TOP_K) * t_total + src
    return (block_e.astype(jnp.int32), n_used.reshape(1).astype(jnp.int32),
            n_valid.astype(jnp.int32), src.astype(jnp.int32), dst.astype(jnp.int32))


def _final_kernel(x1_ref, y0_ref, y1_ref, y2_ref, y3_ref, tg_ref, gt_ref, g2_ref, b2_ref, o_ref,
                  *, alpha):
    tg = tg_ref[...]
    f = tg[:, 0:1] * y0_ref[...]
    for kk, y_ref in enumerate((y1_ref, y2_ref, y3_ref), start=1):
        f = f + tg[:, kk:kk + 1] * y_ref[...]
    o_ref[0] = _ln(alpha * x1_ref[0] + gt_ref[0] * f) * g2_ref[...] + b2_ref[...]


def _final(x1, y4, tg, gt, ln_g, ln_b, alpha, tm, row_off):
    b, s, d = x1.shape
    t_total = tg.shape[0]
    assert TOP_K == 4 and t_total % tm == 0
    nblk = s // tm
    off = row_off // tm
    kblk = t_total // tm
    mrows = gt.shape[1]
    mblk = 1 if mrows == 1 else tm
    mod_spec = pl.BlockSpec((1, mblk, d), (lambda bi, i: (bi, 0, 0)) if mrows == 1
                            else (lambda bi, i: (bi, i, 0)))
    row = pl.BlockSpec((1, tm, d), lambda bi, i: (bi, i, 0))
    flat = lambda w, kk: pl.BlockSpec((tm, w), lambda bi, i: (kk * kblk + off + bi * nblk + i, 0))
    return pl.pallas_call(
        functools.partial(_final_kernel, alpha=alpha),
        grid=(b, nblk),
        in_specs=[row] + [flat(d, kk) for kk in range(TOP_K)]
                 + [flat(LANES, 0), mod_spec, _const_spec((1, d)), _const_spec((1, d))],
        out_specs=row,
        out_shape=jax.ShapeDtypeStruct((b, s, d), F32),
        compiler_params=_cparams(("arbitrary", "arbitrary")),
        name="final",
    )(x1, y4, y4, y4, y4, tg, gt, ln_g.reshape(1, d), ln_b.reshape(1, d))


def _layer(xp, xs, cp, cs, gla_s, pool_c, w_in, w_gate_lr, b_gate_lr, gla_norm_g, w_pool,
           pool_scale, w_out, w_ada, b_ada, ln1_g, ln1_b, ln2_g, ln2_b, router_w, router_b,
           w_gate, b_gate, w_up, b_up, w_down, b_down, alpha):
    bp, sp, d = xp.shape
    bs, ss, _ = xs.shape
    heads = gla_s.shape[1]
    dk, dv = gla_s.shape[2], gla_s.shape[3]
    kw, vw = heads * dk, heads * dv
    rank = w_gate_lr.shape[0]
    pw = pool_c.shape[2]
    n_experts = router_w.shape[1]
    assert dk == LANES and rank <= LANES and n_experts <= LANES

    n_c = bp + bs
    n_pad = -(-n_c // 8) * 8
    c_all = jnp.concatenate([cp, cs, jnp.zeros((n_pad - n_c, d), F32)], axis=0)
    mod = _ada(c_all, w_ada, b_ada)
    mods_p = [mod[:bp, i * d:(i + 1) * d].reshape(bp, 1, d) for i in range(6)]
    ns = bs * ss
    mods_s = [jnp.repeat(mod[bp:n_c, i * d:(i + 1) * d], ss, axis=0).reshape(1, ns, d)
              for i in range(6)]

    o = 2 * kw + 2 * vw
    w2 = jnp.zeros((LANES, kw), F32).at[:rank].set(w_gate_lr)
    w2h, w2l = _split2(w2)
    wglr = jnp.zeros((d, LANES), F32).at[:, :rank].set(w_in[:, o:o + rank]).astype(BF16)
    wts = (w_in[:, :kw].astype(BF16), w_in[:, kw:2 * kw].astype(BF16),
           w_in[:, 2 * kw:2 * kw + vw].astype(BF16), w_in[:, 2 * kw + vw:o].astype(BF16),
           w_in[:, o + rank:].astype(BF16), wglr, w2h, w2l, b_gate_lr.reshape(1, kw))
    w_out_b = w_out.astype(BF16)
    w_pool_b = w_pool.astype(BF16)
    rw = jnp.zeros((d, LANES), F32).at[:, :n_experts].set(router_w)
    rwh, rwl = _split2(rw)
    rb = jnp.full((1, LANES), NEG, F32).at[0, :n_experts].set(router_b)

    t_total = bp * sp + ns
    tm_p = min(512, sp)

    q_scale = float(dk) ** -0.5
    q, k, v, r, u, lg = _inproj(xp, mods_p[0], mods_p[1], wts, tm_p, q_scale)
    og, gla_p = _gla(q, k, v, lg, r, gla_norm_g, None, GLA_CHUNK, tm_p, heads)
    hist0 = jnp.zeros((bp, POOL_HIST + 1, pw), F32)
    o_pool, pool_p = _pool(u, hist0, w_pool_b, pool_scale, 0, tm_p)
    x1p, h2, topi, topg = _outproj(xp, og, o_pool, mods_p[2], mods_p[3], mods_p[4], w_out_b,
                                   ln1_g, ln1_b, rwh, rwl, rb, alpha, tm_p, t_total, 0, None)

    xs_flat = xs.reshape(1, ns, d)
    q, k, v, r, u, lg = _inproj(xs_flat, mods_s[0], mods_s[1], wts, ns, q_scale)
    seq = lambda a: a.reshape(bs, ss, a.shape[-1])
    og, gla_s_new = _gla(seq(q), seq(k), seq(v), seq(lg), seq(r), gla_norm_g, gla_s, ss, ss, heads)
    hist_s = jnp.concatenate([jnp.zeros((bs, 1, pw), F32), pool_c], axis=1)
    o_pool, pool_s_new = _pool(seq(u), hist_s, w_pool_b, pool_scale, PAST_LEN, ss)
    x1s, h2, topi, topg = _outproj(xs_flat, og.reshape(1, ns, vw), o_pool.reshape(1, ns, pw),
                                   mods_s[2], mods_s[3], mods_s[4], w_out_b, ln1_g, ln1_b,
                                   rwh, rwl, rb, alpha, ns, t_total, bp * sp, (h2, topi, topg))

    block_e, n_used, n_valid, src, dst = _route(topi, MOE_BLK, n_experts)
    y4 = _moe(h2, block_e, n_used, n_valid, src, dst, w_gate.astype(BF16), b_gate,
              w_up.astype(BF16), b_up, w_down.astype(BF16), b_down)
    tm_f = min(256, sp)
    yp = _final(x1p, y4, topg, mods_p[5], ln2_g, ln2_b, alpha, tm_f, 0)
    ys = _final(x1s, y4, topg, mods_s[5], ln2_g, ln2_b, alpha, ns, bp * sp)
    return yp, ys.reshape(bs, ss, d), gla_p, gla_s_new, pool_p, pool_s_new


def kernel(x_prompt, x_sample, c_prompt, c_sample, state_gla, cache_pool, w_in, w_gate_lr, b_gate_lr, gla_norm_g, w_pool, pool_scale, w_out, w_ada, b_ada, ln1_g, ln1_b, ln2_g, ln2_b, router_w, router_b, w_gate, b_gate, w_up, b_up, w_down, b_down):
    depth = w_in.shape[0]
    alpha = float((2 * depth) ** 0.25)
    xp, xs = x_prompt, x_sample
    gla_p_all, gla_s_all, pool_p_all, pool_s_all = [], [], [], []
    for l in range(depth):
        xp, xs, gla_p, gla_s, pool_p, pool_s = _layer(
            xp, xs, c_prompt, c_sample, state_gla[l], cache_pool[l], w_in[l], w_gate_lr[l],
            b_gate_lr[l], gla_norm_g[l], w_pool[l], pool_scale[l], w_out[l], w_ada[l], b_ada[l],
            ln1_g[l], ln1_b[l], ln2_g[l], ln2_b[l], router_w[l], router_b[l], w_gate[l], b_gate[l],
            w_up[l], b_up[l], w_down[l], b_down[l], alpha)
        gla_p_all.append(gla_p)
        gla_s_all.append(gla_s)
        pool_p_all.append(pool_p)
        pool_s_all.append(pool_s)
    return (xp, xs, jnp.stack(gla_p_all, 0), jnp.stack(gla_s_all, 0),
            jnp.stack(pool_p_all, 0), jnp.stack(pool_s_all, 0))
```
